```python
import jax, jax.numpy as jnp
from jax import lax
import numpy as np

D_MODEL = 1024
BATCH = 16
SEQ = 2048
DEPTH = 2
DEC_BATCH = 32
DEC_SEQ = 8
PAST_LEN = 16384
PAGE_SIZE = 128

N_A_LAYERS = DEPTH // 2
N_B_LAYERS = DEPTH - N_A_LAYERS
D_FF = 2816
CONV_W = 31
N_HEADS = 16
HEAD_DIM = D_MODEL // N_HEADS
N_KV = 4
GROUP = N_HEADS // N_KV
CMP_STRIDE = 16
L_CMP = 2 * CMP_STRIDE
CMP_HID = 128
SEL_BLK = 64
N_SELECT = 16
WINDOW = 512
W_BLK = 128
N_BAND = WINDOW // W_BLK + 1
SEL_QBLK = 16
N_KV_ROWS = 6
N_PAGED = 4
EPS = 1e-6
NEG = -1e30
FORCE = 1e9
F32 = jnp.float32

kernel_name = 'yoco_conformer_conv_nsa_step'


def rmsnorm(x, g):
    xf = x.astype(F32)
    y = xf * lax.rsqrt(jnp.mean(xf * xf, axis=-1, keepdims=True) + EPS)
    return (y * g.astype(F32)).astype(x.dtype)


def swiglu(h, w_in, w_out):
    a, b = jnp.split(h @ w_in, 2, axis=-1)
    return (jax.nn.silu(a) * b) @ w_out


def masked_softmax(s, mask):
    s = jnp.where(mask, s, NEG)
    m = jnp.max(s, axis=-1, keepdims=True)
    e = jnp.exp(s - m) * mask
    return e / jnp.maximum(jnp.sum(e, axis=-1, keepdims=True), 1e-30)


def conv_module(h, prev, w_pw1, w_dw, b_dw, ln_g, ln_b, w_pw2):
    a, b = jnp.split(h @ w_pw1, 2, axis=-1)
    v = a * jax.nn.sigmoid(b)
    full = jnp.concatenate([prev.astype(v.dtype), v], axis=1)
    c = lax.conv_general_dilated(full, w_dw[:, None, :].astype(full.dtype), (1,), 'VALID',
                                 dimension_numbers=('NWC', 'WIO', 'NWC'),
                                 feature_group_count=D_MODEL) + b_dw
    cf = c.astype(F32)
    mu = jnp.mean(cf, axis=-1, keepdims=True)
    var = jnp.mean(jnp.square(cf - mu), axis=-1, keepdims=True)
    n = ((cf - mu) * lax.rsqrt(var + EPS) * ln_g.astype(F32) + ln_b.astype(F32)).astype(c.dtype)
    return jax.nn.silu(n) @ w_pw2, full[:, full.shape[1] - (CONV_W - 1):]


def compress(rows, w1, b1, w2, b2, pe):
    B, T = rows.shape[:2]
    n16 = T // CMP_STRIDE
    ch = rows[:, :n16 * CMP_STRIDE].reshape(B, n16, CMP_STRIDE, N_KV, HEAD_DIM)
    first = jnp.einsum('bnsgd,sdh->bngh', ch, w1[:CMP_STRIDE])
    second = jnp.einsum('bnsgd,sdh->bngh', ch, w1[CMP_STRIDE:])
    z = first[:, :-1] + second[:, 1:] + (b1 + jnp.einsum('sd,sdh->h', pe, w1))
    return jax.nn.gelu(z) @ w2 + b2


def build_shared(kc_rows, vc_rows, ks_rows, vs_rows, w_cmp1, b_cmp1, w_cmp2, b_cmp2, pe_cmp):
    kc = compress(kc_rows, w_cmp1[0], b_cmp1[0], w_cmp2[0], b_cmp2[0], pe_cmp[0])
    vc = compress(vc_rows, w_cmp1[1], b_cmp1[1], w_cmp2[1], b_cmp2[1], pe_cmp[1])
    B, T = ks_rows.shape[:2]
    n_sel = -(-T // SEL_BLK)

    def blocks(r):
        r = jnp.pad(r, ((0, 0), (0, n_sel * SEL_BLK - T), (0, 0), (0, 0)))
        return r.reshape(B, n_sel, SEL_BLK, N_KV, HEAD_DIM).transpose(0, 3, 1, 2, 4)

    return {'kc': kc, 'vc': vc, 'kb': blocks(ks_rows), 'vb': blocks(vs_rows)}


def select_attend(q, qpos, idx, kb, vb, slopes):
    B = q.shape[0]
    scale = HEAD_DIM ** -0.5
    bi = jnp.arange(B)[:, None, None, None]
    gi = jnp.arange(N_KV)[None, :, None, None]
    kg = kb[bi, gi, idx]
    vg = vb[bi, gi, idx]
    kpos = idx[..., None] * SEL_BLK + jnp.arange(SEL_BLK)
    dist = (qpos[:, None, None] - kpos)[:, :, None]
    s = (jnp.einsum('bgzqd,bgqkld->bgzqkl', q, kg).astype(F32) * scale
         - slopes[None, :, :, None, None, None] * dist.astype(F32))
    shp = s.shape
    mask = (dist >= 0).reshape(B, N_KV, 1, shp[3], -1)
    p = masked_softmax(s.reshape(shp[0], shp[1], shp[2], shp[3], -1), mask).reshape(shp)
    return jnp.einsum('bgzqkl,bgqkld->bgzqd', p.astype(q.dtype), vg)


def window_banded(q, kw, vw, slopes):
    B, _, _, T, _ = q.shape
    scale = HEAD_DIM ** -0.5
    nb = T // W_BLK
    pad = ((0, 0), (WINDOW, 0), (0, 0), (0, 0))
    kp = jnp.pad(kw, pad).reshape(B, nb + N_BAND - 1, W_BLK, N_KV, HEAD_DIM)
    vp = jnp.pad(vw, pad).reshape(B, nb + N_BAND - 1, W_BLK, N_KV, HEAD_DIM)
    kband = jnp.concatenate([kp[:, i:i + nb] for i in range(N_BAND)], axis=2)
    vband = jnp.concatenate([vp[:, i:i + nb] for i in range(N_BAND)], axis=2)
    qb = q.reshape(B, N_KV, GROUP, nb, W_BLK, HEAD_DIM)
    r = jnp.arange(N_BAND * W_BLK)[None, :]
    dist = jnp.arange(W_BLK)[:, None] - r + WINDOW
    kpos = jnp.arange(nb)[:, None] * W_BLK - WINDOW + r
    mask = (dist >= 0) & (dist < WINDOW) & (kpos[:, None, :] >= 0)
    s = (jnp.einsum('bgzncd,bnkgd->bgznck', qb, kband).astype(F32) * scale
         - slopes[:, :, None, None, None] * dist.astype(F32))
    p = masked_softmax(s, mask)
    o = jnp.einsum('bgznck,bnkgd->bgzncd', p.astype(q.dtype), vband)
    return o.reshape(B, N_KV, GROUP, T, HEAD_DIM)


def window_dense(q, qpos, kw, vw, kpos, slopes):
    scale = HEAD_DIM ** -0.5
    dist = qpos[:, None] - kpos[None, :]
    mask = (dist >= 0) & (dist < WINDOW)
    s = (jnp.einsum('bgzqd,bkgd->bgzqk', q, kw).astype(F32) * scale
         - slopes[:, :, None, None] * dist.astype(F32))
    p = masked_softmax(s, mask)
    return jnp.einsum('bgzqk,bkgd->bgzqd', p.astype(q.dtype), vw)


def nsa_mixer(h, qpos, sh, w_qg, w_o, slopes, prompt):
    B, T = h.shape[:2]
    scale = HEAD_DIM ** -0.5
    proj = h @ w_qg
    q = proj[..., :N_HEADS * HEAD_DIM].reshape(B, T, N_KV, GROUP, HEAD_DIM).transpose(0, 2, 3, 1, 4)
    gates = jax.nn.sigmoid(proj[..., N_HEADS * HEAD_DIM:].astype(F32))
    gates = gates.reshape(B, T, 3, N_KV, GROUP).transpose(2, 0, 3, 4, 1)[..., None]
    kc, vc = sh['kc'], sh['vc']
    n_cmp = kc.shape[1]
    cend = jnp.arange(n_cmp) * CMP_STRIDE + (L_CMP - 1)
    dist_c = qpos[:, None] - cend[None, :]
    s_c = (jnp.einsum('bgzqd,bngd->bgzqn', q, kc).astype(F32) * scale
           - slopes[:, :, None, None] * dist_c.astype(F32))
    p_c = masked_softmax(s_c, dist_c >= 0)
    o_c = jnp.einsum('bgzqn,bngd->bgzqd', p_c.astype(q.dtype), vc)
    kb, vb = sh['kb'], sh['vb']
    n_sel = kb.shape[2]
    cstart = jnp.arange(n_cmp)[:, None] * CMP_STRIDE
    sstart = jnp.arange(n_sel)[None, :] * SEL_BLK
    overlap = jnp.clip(jnp.minimum(cstart + L_CMP, sstart + SEL_BLK) - jnp.maximum(cstart, sstart),
                       0, L_CMP).astype(F32) / L_CMP
    imp = jnp.einsum('bgqn,nj->bgqj', p_c.sum(axis=2), overlap)
    jj = jnp.arange(n_sel)[None, :]
    cur = (qpos // SEL_BLK)[:, None]
    forced = (jj == 0) | (jj == cur) | (jj == cur - 1)
    causal = jj * SEL_BLK <= qpos[:, None]
    score = jnp.where(forced, FORCE, jnp.where(causal, imp, NEG))
    _, idx = lax.top_k(score, min(N_SELECT, n_sel))
    if prompt:
        nC = T // SEL_QBLK
        k = idx.shape[-1]
        qc = q.reshape(B, N_KV, GROUP, nC, SEL_QBLK, HEAD_DIM).transpose(3, 0, 1, 2, 4, 5)
        pc = qpos.reshape(nC, SEL_QBLK)
        ic = idx.reshape(B, N_KV, nC, SEL_QBLK, k).transpose(2, 0, 1, 3, 4)
        o_s = lax.map(lambda a: select_attend(a[0], a[1], a[2], kb, vb, slopes), (qc, pc, ic))
        o_s = o_s.transpose(1, 2, 3, 0, 4, 5).reshape(B, N_KV, GROUP, T, HEAD_DIM)
        o_w = window_banded(q, sh['kw'], sh['vw'], slopes)
    else:
        o_s = select_attend(q, qpos, idx, kb, vb, slopes)
        o_w = window_dense(q, qpos, sh['kw'], sh['vw'], sh['kpos'], slopes)
    o = (gates[0] * o_c + gates[1] * o_s + gates[2] * o_w).astype(h.dtype)
    o = o.transpose(0, 3, 1, 2, 4).reshape(B, T, N_HEADS * HEAD_DIM)
    return o @ w_o


def run_trunk(x, qpos, conv_prev, make_shared, W, slopes, prompt):
    conv_new = []
    shared = None
    extra = None
    for layer in range(DEPTH):
        if layer == N_A_LAYERS:
            shared, extra = make_shared(x)
        g = W['norm_g'][layer]
        x = x + 0.5 * swiglu(rmsnorm(x, g[0]), W['w_ff_in'][layer, 0], W['w_ff_out'][layer, 0])
        if layer < N_A_LAYERS:
            a = layer
            out, st = conv_module(rmsnorm(x, g[1]), conv_prev[a], W['w_pw1'][a], W['w_dw'][a],
                                  W['b_dw'][a], W['ln_g'][a], W['ln_b'][a], W['w_pw2'][a])
            conv_new.append(st)
        else:
            b = layer - N_A_LAYERS
            out = nsa_mixer(rmsnorm(x, g[1]), qpos, shared, W['w_qg'][b], W['w_o'][b], slopes, prompt)
        x = x + out
        x = x + 0.5 * swiglu(rmsnorm(x, g[2]), W['w_ff_in'][layer, 1], W['w_ff_out'][layer, 1])
    return rmsnorm(x, W['g_final']), jnp.stack(conv_new), extra


def setup_inputs(seed: int = 0) -> dict:
    key = jax.random.key(seed)
    ks = jax.random.split(key, 32)
    n_pages = PAST_LEN // PAGE_SIZE
    n_pool = (5 * DEC_BATCH * n_pages) // 4
    wbuf = min(WINDOW, PAST_LEN)

    def nrm(i, shape, scale=1.0):
        return jax.random.normal(ks[i], shape, F32) * scale

    page_table = jax.random.permutation(ks[3], n_pool)[:DEC_BATCH * n_pages]
    page_table = page_table.reshape(DEC_BATCH, n_pages).astype(jnp.int32)
    return {
        'x_prompt': nrm(0, (BATCH, SEQ, D_MODEL)),
        'x_sample': nrm(1, (DEC_BATCH, DEC_SEQ, D_MODEL)),
        'cache_kv': nrm(2, (n_pool, PAGE_SIZE, N_PAGED, N_KV, HEAD_DIM)),
        'page_table': page_table,
        'state_kv_win': nrm(4, (DEC_BATCH, wbuf, N_KV_ROWS - N_PAGED, N_KV, HEAD_DIM)),
        'state_conv': nrm(5, (N_A_LAYERS, DEC_BATCH, CONV_W - 1, D_MODEL), 0.5),
        'norm_g': 1.0 + nrm(6, (DEPTH, 3, D_MODEL), 0.02),
        'w_ff_in': nrm(7, (DEPTH, 2, D_MODEL, 2 * D_FF), D_MODEL ** -0.5),
        'w_ff_out': nrm(8, (DEPTH, 2, D_FF, D_MODEL), D_FF ** -0.5),
        'w_pw1': nrm(9, (N_A_LAYERS, D_MODEL, 2 * D_MODEL), D_MODEL ** -0.5),
        'w_dw': nrm(10, (N_A_LAYERS, CONV_W, D_MODEL), CONV_W ** -0.5),
        'b_dw': nrm(11, (N_A_LAYERS, D_MODEL), 0.01),
        'ln_g': 1.0 + nrm(12, (N_A_LAYERS, D_MODEL), 0.02),
        'ln_b': nrm(13, (N_A_LAYERS, D_MODEL), 0.01),
        'w_pw2': nrm(14, (N_A_LAYERS, D_MODEL, D_MODEL), D_MODEL ** -0.5),
        'g_kv': 1.0 + nrm(15, (D_MODEL,), 0.02),
        'w_kv': nrm(16, (D_MODEL, N_KV_ROWS * N_KV * HEAD_DIM), D_MODEL ** -0.5),
        'w_cmp1': nrm(17, (2, L_CMP, HEAD_DIM, CMP_HID), (L_CMP * HEAD_DIM) ** -0.5),
        'b_cmp1': nrm(18, (2, CMP_HID), 0.01),
        'w_cmp2': nrm(19, (2, CMP_HID, HEAD_DIM), CMP_HID ** -0.5),
        'b_cmp2': nrm(20, (2, HEAD_DIM), 0.01),
        'pe_cmp': nrm(21, (2, L_CMP, HEAD_DIM), 0.1),
        'w_qg': nrm(22, (N_B_LAYERS, D_MODEL, N_HEADS * HEAD_DIM + 3 * N_HEADS), D_MODEL ** -0.5),
        'w_o': nrm(23, (N_B_LAYERS, N_HEADS * HEAD_DIM, D_MODEL), (N_HEADS * HEAD_DIM) ** -0.5),
        'g_final': 1.0 + nrm(24, (D_MODEL,), 0.02),
    }


def reference(x_prompt, x_sample, cache_kv, page_table, state_kv_win, state_conv,
              norm_g, w_ff_in, w_ff_out, w_pw1, w_dw, b_dw, ln_g, ln_b, w_pw2,
              g_kv, w_kv, w_cmp1, b_cmp1, w_cmp2, b_cmp2, pe_cmp, w_qg, w_o, g_final):
    W = {'norm_g': norm_g, 'w_ff_in': w_ff_in, 'w_ff_out': w_ff_out, 'w_pw1': w_pw1,
         'w_dw': w_dw, 'b_dw': b_dw, 'ln_g': ln_g, 'ln_b': ln_b, 'w_pw2': w_pw2,
         'w_qg': w_qg, 'w_o': w_o, 'g_final': g_final}
    slopes = jnp.exp2(-8.0 * (jnp.arange(N_HEADS, dtype=F32) + 1.0) / N_HEADS).reshape(N_KV, GROUP)

    def kv_rows(x):
        B, T = x.shape[:2]
        return (rmsnorm(x, g_kv) @ w_kv).reshape(B, T, N_KV_ROWS, N_KV, HEAD_DIM)

    def shared_prompt(x):
        rows = kv_rows(x)
        T = rows.shape[1]
        sh = build_shared(rows[:, :, 0], rows[:, :, 1], rows[:, :, 2], rows[:, :, 3],
                          w_cmp1, b_cmp1, w_cmp2, b_cmp2, pe_cmp)
        sh['kw'] = rows[:, :, 4]
        sh['vw'] = rows[:, :, 5]
        keep = min(WINDOW, T)
        return sh, (rows[:, :, :N_PAGED], rows[:, T - keep:, N_PAGED:])

    qpos_p = jnp.arange(x_prompt.shape[1])
    conv0 = jnp.zeros((N_A_LAYERS, x_prompt.shape[0], CONV_W - 1, D_MODEL), x_prompt.dtype)
    y_prompt, conv_prompt, ext_p = run_trunk(x_prompt, qpos_p, conv0, shared_prompt, W, slopes, True)
    kv_prompt, win_prompt = ext_p

    past_len = page_table.shape[1] * PAGE_SIZE

    def shared_sample(x):
        rows = kv_rows(x)
        Bd, Tn = rows.shape[:2]

        def full(c):
            past = cache_kv[page_table, :, c].reshape(Bd, past_len, N_KV, HEAD_DIM).astype(rows.dtype)
            return jnp.concatenate([past, rows[:, :, c]], axis=1)

        sh = build_shared(full(0), full(1), full(2), full(3),
                          w_cmp1, b_cmp1, w_cmp2, b_cmp2, pe_cmp)
        win = jnp.concatenate([state_kv_win.astype(rows.dtype), rows[:, :, N_PAGED:]], axis=1)
        wbuf = state_kv_win.shape[1]
        sh['kw'] = win[:, :, 0]
        sh['vw'] = win[:, :, 1]
        sh['kpos'] = past_len - wbuf + jnp.arange(wbuf + Tn)
        keep = min(WINDOW, past_len + Tn)
        return sh, (rows[:, :, :N_PAGED], win[:, win.shape[1] - keep:])

    qpos_s = past_len + jnp.arange(x_sample.shape[1])
    y_sample, conv_sample, ext_s = run_trunk(x_sample, qpos_s, state_conv, shared_sample, W, slopes, False)
    kv_sample, win_sample = ext_s

    return (y_prompt, y_sample, kv_prompt, kv_sample, win_prompt, win_sample, conv_prompt, conv_sample)
```

```python
import functools

import jax
import jax.numpy as jnp
from jax import lax
from jax.experimental import pallas as pl
from jax.experimental.pallas import tpu as pltpu

F32 = jnp.float32
BF16 = jnp.bfloat16

D_MODEL = 1024
D_FF = 2816
CONV_W = 31
N_HEADS = 16
HEAD_DIM = 64
N_KV = 4
GROUP = 4
CMP_STRIDE = 16
L_CMP = 32
CMP_HID = 128
SEL_BLK = 64
N_SELECT = 16
WINDOW = 512
PAGE_SIZE = 128
N_KV_ROWS = 6
N_PAGED = 4
EPS = 1e-6
NEG = -1e30
FORCE = 1e9

LANES = 128
SUBLANES = 8
KV_W = N_KV * HEAD_DIM
HALO = 32
PAGES_PER_STEP = 16
CHUNK = 128
TQ = 128
VMEM_LIMIT = 56 * 1024 * 1024


def _params(*sem):
    return pltpu.CompilerParams(dimension_semantics=sem, vmem_limit_bytes=VMEM_LIMIT)


def _const_spec(shape):
    nd = len(shape)
    return pl.BlockSpec(shape, lambda *_: (0,) * nd, pipeline_mode=pl.Buffered(1))


def _rms(x, g):
    return x * lax.rsqrt(jnp.mean(x * x, axis=-1, keepdims=True) + EPS) * g


def _dot(a, b):
    return jnp.dot(a, b, preferred_element_type=F32)


def _dot_nt(a, b):
    return lax.dot_general(a, b, (((1,), (1,)), ((), ())), preferred_element_type=F32)


def _masked_softmax(s, valid):
    s = jnp.where(valid, s, NEG)
    m = jnp.max(s, axis=-1, keepdims=True)
    e = jnp.where(valid, jnp.exp(s - m), 0.0)
    return e / jnp.maximum(jnp.sum(e, axis=-1, keepdims=True), 1e-30)


def _dot_f32_exact_rhs(x, w_bf16, rows):
    hi = x.astype(BF16)
    r1 = x - hi.astype(F32)
    mid = r1.astype(BF16)
    lo = (r1 - mid.astype(F32)).astype(BF16)
    y = _dot(jnp.concatenate([hi, mid, lo], axis=0), w_bf16)
    return (y[2 * rows:3 * rows] + y[rows:2 * rows]) + y[0:rows]


FF_CHUNK = D_FF // 2


def _ffn_kernel(x_ref, g_ref, win_ref, wout_ref, gf_ref, o_ref, *, final):
    x = x_ref[...]
    xn = _rms(x, g_ref[...]).astype(BF16)
    acc = jnp.zeros_like(x)
    for c in range(D_FF // FF_CHUNK):
        lo = c * FF_CHUNK
        a = _dot(xn, win_ref[:, lo:lo + FF_CHUNK])
        b = _dot(xn, win_ref[:, D_FF + lo:D_FF + lo + FF_CHUNK])
        h = (a * jax.nn.sigmoid(a) * b).astype(BF16)
        acc = acc + _dot(h, wout_ref[lo:lo + FF_CHUNK, :])
    y = x + 0.5 * acc
    if final:
        y = _rms(y, gf_ref[...])
    o_ref[...] = y


def _row_tile(rows, pref):
    tm = min(rows, pref)
    while rows % tm:
        tm //= 2
    assert tm % SUBLANES == 0, (rows, tm)
    return tm


def _ffn(x, g, w_in, w_out, g_final=None):
    rows = x.shape[0]
    tm = _row_tile(rows, 512)
    final = g_final is not None
    gf = g_final if final else g
    row = pl.BlockSpec((tm, D_MODEL), lambda i: (i, 0))
    return pl.pallas_call(
        functools.partial(_ffn_kernel, final=final),
        grid=(rows // tm,),
        in_specs=[row, _const_spec((1, D_MODEL)), _const_spec((D_MODEL, 2 * D_FF)),
                  _const_spec((D_FF, D_MODEL)), _const_spec((1, D_MODEL))],
        out_specs=row,
        out_shape=jax.ShapeDtypeStruct((rows, D_MODEL), F32),
        compiler_params=_params("parallel"),
        name="ffn",
    )(x, g.reshape(1, D_MODEL), w_in, w_out, gf.reshape(1, D_MODEL))


CONV_ROWS = 32
CONV_LANES = 256


def _conv_kernel(x_ref, prev_ref, g_ref, wpw1_ref, wdw_ref, bdw_ref, lng_ref, lnb_ref, wpw2_ref,
                 y_ref, st_ref, vbuf, cbuf, *, tt, nt):
    t = pl.program_id(1)

    @pl.when(t == 0)
    def _():
        vbuf[0:HALO, :] = prev_ref[0]

    @pl.when(t > 0)
    def _():
        vbuf[0:HALO, :] = vbuf[tt:tt + HALO, :]

    x = x_ref[0]
    xn = _rms(x, g_ref[...]).astype(BF16)
    ab = _dot(xn, wpw1_ref[...])
    vbuf[HALO:HALO + tt, :] = ab[:, :D_MODEL] * jax.nn.sigmoid(ab[:, D_MODEL:])

    base = HALO - (CONV_W - 1)
    rc = min(tt, CONV_ROWS)
    for l0 in range(0, D_MODEL, CONV_LANES):
        lanes = slice(l0, l0 + CONV_LANES)
        for r0 in range(0, tt, rc):
            acc = jnp.broadcast_to(bdw_ref[:, lanes], (rc, CONV_LANES))
            for k in range(CONV_W):
                acc = acc + wdw_ref[k:k + 1, lanes] * vbuf[base + k + r0:base + k + r0 + rc, lanes]
            cbuf[r0:r0 + rc, lanes] = acc

    c = cbuf[...]
    mu = jnp.mean(c, axis=-1, keepdims=True)
    cc = c - mu
    var = jnp.mean(cc * cc, axis=-1, keepdims=True)
    n = cc * lax.rsqrt(var + EPS) * lng_ref[...] + lnb_ref[...]
    h = (n * jax.nn.sigmoid(n)).astype(BF16)
    y_ref[0] = x + _dot(h, wpw2_ref[...])

    @pl.when(t == nt - 1)
    def _():
        st_ref[0] = vbuf[tt:tt + HALO, :]


def _conv(x, prev, g, w_pw1, w_dw, b_dw, ln_g, ln_b, w_pw2):
    bsz, seq, _ = x.shape
    tt = _row_tile(seq, 256)
    nt = seq // tt
    prev32 = jnp.pad(prev, ((0, 0), (HALO - (CONV_W - 1), 0), (0, 0)))
    vec = lambda a: a.reshape(1, D_MODEL)
    y, st = pl.pallas_call(
        functools.partial(_conv_kernel, tt=tt, nt=nt),
        grid=(bsz, nt),
        in_specs=[pl.BlockSpec((1, tt, D_MODEL), lambda b, t: (b, t, 0)),
                  pl.BlockSpec((1, HALO, D_MODEL), lambda b, t: (b, 0, 0)),
                  _const_spec((1, D_MODEL)), _const_spec((D_MODEL, 2 * D_MODEL)),
                  _const_spec((CONV_W, D_MODEL)), _const_spec((1, D_MODEL)),
                  _const_spec((1, D_MODEL)), _const_spec((1, D_MODEL)),
                  _const_spec((D_MODEL, D_MODEL))],
        out_specs=[pl.BlockSpec((1, tt, D_MODEL), lambda b, t: (b, t, 0)),
                   pl.BlockSpec((1, HALO, D_MODEL), lambda b, t: (b, 0, 0))],
        out_shape=[jax.ShapeDtypeStruct((bsz, seq, D_MODEL), F32),
                   jax.ShapeDtypeStruct((bsz, HALO, D_MODEL), F32)],
        scratch_shapes=[pltpu.VMEM((HALO + tt, D_MODEL), F32), pltpu.VMEM((tt, D_MODEL), F32)],
        compiler_params=_params("parallel", "arbitrary"),
        name="conv",
    )(x, prev32, vec(g), w_pw1, w_dw, vec(b_dw), vec(ln_g), vec(ln_b), w_pw2)
    return y, st[:, HALO - (CONV_W - 1):]


N_KV_COLS = N_KV_ROWS * KV_W
N_PAGED_COLS = N_PAGED * KV_W


def _kv_kernel(x_ref, g_ref, w_ref, kv_ref, win_ref, kvb_ref):
    y = _dot(_rms(x_ref[...], g_ref[...]).astype(BF16), w_ref[...])
    kv_ref[...] = y[:, :N_PAGED_COLS]
    win_ref[...] = y[:, N_PAGED_COLS:]
    kvb_ref[...] = y.astype(BF16)


def _kv_rows(x, g, w):
    rows = x.shape[0]
    tm = _row_tile(rows, 512)
    return pl.pallas_call(
        _kv_kernel,
        grid=(rows // tm,),
        in_specs=[pl.BlockSpec((tm, D_MODEL), lambda i: (i, 0)), _const_spec((1, D_MODEL)),
                  _const_spec((D_MODEL, N_KV_COLS))],
        out_specs=[pl.BlockSpec((tm, N_PAGED_COLS), lambda i: (i, 0)),
                   pl.BlockSpec((tm, N_KV_COLS - N_PAGED_COLS), lambda i: (i, 0)),
                   pl.BlockSpec((tm, N_KV_COLS), lambda i: (i, 0))],
        out_shape=[jax.ShapeDtypeStruct((rows, N_PAGED_COLS), F32),
                   jax.ShapeDtypeStruct((rows, N_KV_COLS - N_PAGED_COLS), F32),
                   jax.ShapeDtypeStruct((rows, N_KV_COLS), BF16)],
        compiler_params=_params("parallel"),
        name="kv_rows",
    )(x, g.reshape(1, D_MODEL), w)


N_Q_COLS = N_HEADS * HEAD_DIM
N_GATE = 3 * N_HEADS
Q_SCALE = HEAD_DIM ** -0.5


def _qg_kernel(x_ref, g_ref, w_ref, q_ref, gate_ref):
    y = _dot(_rms(x_ref[...], g_ref[...]).astype(BF16), w_ref[...])
    q_ref[...] = (y[:, :N_Q_COLS] * Q_SCALE).astype(BF16)
    gate_ref[...] = jax.nn.sigmoid(y[:, N_Q_COLS:])


def _qg_proj(x, g, w):
    rows = x.shape[0]
    tm = _row_tile(rows, 512)
    ncol = w.shape[1]
    return pl.pallas_call(
        _qg_kernel,
        grid=(rows // tm,),
        in_specs=[pl.BlockSpec((tm, D_MODEL), lambda i: (i, 0)), _const_spec((1, D_MODEL)),
                  _const_spec((D_MODEL, ncol))],
        out_specs=[pl.BlockSpec((tm, N_Q_COLS), lambda i: (i, 0)),
                   pl.BlockSpec((tm, LANES), lambda i: (i, 0))],
        out_shape=[jax.ShapeDtypeStruct((rows, N_Q_COLS), BF16),
                   jax.ShapeDtypeStruct((rows, LANES), F32)],
        compiler_params=_params("parallel"),
        name="qg_proj",
    )(x, g.reshape(1, D_MODEL), w)


def _oproj_kernel(o_ref, w_ref, x_ref, y_ref):
    y_ref[...] = x_ref[...] + _dot(o_ref[...], w_ref[...])


def _oproj(o, w, x):
    rows = x.shape[0]
    tm = _row_tile(rows, 512)
    row = pl.BlockSpec((tm, D_MODEL), lambda i: (i, 0))
    return pl.pallas_call(
        _oproj_kernel,
        grid=(rows // tm,),
        in_specs=[pl.BlockSpec((tm, N_Q_COLS), lambda i: (i, 0)), _const_spec((N_Q_COLS, D_MODEL)), row],
        out_specs=row,
        out_shape=jax.ShapeDtypeStruct((rows, D_MODEL), F32),
        compiler_params=_params("parallel"),
        name="oproj",
    )(o, w, x)


CHUNKS_PER_PAGE = PAGE_SIZE // CMP_STRIDE
FS_W = N_KV * 2 * CMP_HID
PAGE_HALVES = KV_W // LANES
FS_HALF = FS_W // PAGE_HALVES


def _compress_kernel(pt_ref, *refs, n16, n_steps):
    del pt_ref
    n_in = PAGES_PER_STEP * PAGE_HALVES
    pages = refs[:n_in]
    wpair_ref, pe_ref, w1f_ref, b1_ref, w2bd_ref, b2_ref, out_ref, fs_ref = refs[n_in:]
    st = pl.program_id(2)
    rows_step = PAGES_PER_STEP * CHUNKS_PER_PAGE
    row0 = pl.multiple_of(st * rows_step, rows_step)

    @pl.when(st == 0)
    def _():
        fs_ref[n16:n16 + SUBLANES, :] = jnp.zeros((SUBLANES, FS_W), F32)

    for hf in range(PAGE_HALVES):
        acc = jnp.zeros((rows_step, FS_HALF), F32)
        for s in range(CMP_STRIDE):
            xs = jnp.concatenate(
                [pages[i * PAGE_HALVES + hf][0, pl.ds(s, CHUNKS_PER_PAGE, stride=CMP_STRIDE), :]
                 for i in range(PAGES_PER_STEP)], axis=0)
            acc = acc + _dot(xs.astype(BF16), wpair_ref[0, s])
        fs_ref[pl.ds(row0, rows_step), hf * FS_HALF:(hf + 1) * FS_HALF] = acc

    @pl.when(st == n_steps - 1)
    def _():
        bz = b1_ref[0] + _dot(pe_ref[0], w1f_ref[0])
        zs = []
        for g in range(N_KV):
            f = fs_ref[0:n16, g * 2 * CMP_HID:g * 2 * CMP_HID + CMP_HID]
            sec = fs_ref[1:n16 + 1, g * 2 * CMP_HID + CMP_HID:(g + 1) * 2 * CMP_HID]
            zs.append(jax.nn.gelu(f + sec + bz).astype(BF16))
        z = jnp.concatenate(zs, axis=1)
        out_ref[0, 0] = _dot(z, w2bd_ref[0]) + b2_ref[0]


def _compress(pages, page_table, ch_offset, wpair, pe, w1f, b1, w2bd, b2):
    bsz, n_pages = page_table.shape
    assert n_pages % PAGES_PER_STEP == 0
    n_steps = n_pages // PAGES_PER_STEP
    n16 = n_pages * CHUNKS_PER_PAGE

    def page_spec(i, hf):
        return pl.BlockSpec(
            (1, PAGE_SIZE, LANES),
            lambda ch, b, st, pt: (pt[b, st * PAGES_PER_STEP + i], 0, (ch_offset + ch) * PAGE_HALVES + hf))

    per_ch = lambda shape: pl.BlockSpec((1,) + shape, lambda ch, b, st, pt: (ch,) + (0,) * len(shape))
    grid_spec = pltpu.PrefetchScalarGridSpec(
        num_scalar_prefetch=1,
        grid=(2, bsz, n_steps),
        in_specs=[page_spec(i, hf) for i in range(PAGES_PER_STEP) for hf in range(PAGE_HALVES)] + [
            per_ch((CMP_STRIDE, LANES, FS_HALF)), per_ch((1, L_CMP * HEAD_DIM)),
            per_ch((L_CMP * HEAD_DIM, CMP_HID)), per_ch((1, CMP_HID)),
            per_ch((N_KV * CMP_HID, KV_W)), per_ch((1, KV_W))],
        out_specs=pl.BlockSpec((1, 1, n16, KV_W), lambda ch, b, st, pt: (ch, b, 0, 0)),
        scratch_shapes=[pltpu.VMEM((n16 + SUBLANES, FS_W), F32)],
    )
    return pl.pallas_call(
        functools.partial(_compress_kernel, n16=n16, n_steps=n_steps),
        grid_spec=grid_spec,
        out_shape=jax.ShapeDtypeStruct((2, bsz, n16, KV_W), F32),
        compiler_params=_params("arbitrary", "arbitrary", "arbitrary"),
        name="compress",
    )(page_table, *([pages] * (PAGES_PER_STEP * PAGE_HALVES)), wpair, pe, w1f, b1, w2bd, b2)


def _select_mask(imp, qpos, n_sel, width, rows):
    jj = lax.broadcasted_iota(jnp.int32, (rows, width), 1)
    cur = qpos // SEL_BLK
    forced = (jj == 0) | (jj == cur) | (jj == cur - 1)
    causal = jj * SEL_BLK <= qpos
    score = jnp.where(forced, FORCE, jnp.where(causal, imp, NEG))
    score = jnp.where(jj < n_sel, score, -3e38)

    def body(i, rank):
        ci = jnp.sum(jnp.where(jj == i, score, 0.0), axis=-1, keepdims=True)
        beats = (ci > score) | ((ci == score) & (i < jj))
        return rank + jnp.where(beats, 1.0, 0.0)

    rank = lax.fori_loop(0, n_sel, body, jnp.zeros((rows, width), F32))
    return jnp.where(rank < float(min(N_SELECT, n_sel)), 1.0, 0.0)


def _attn_prompt_kernel(slopes_ref, q_ref, gate_ref, kc_ref, vc_ref, ks_ref, vs_ref, kw_ref, vw_ref, ov_ref,
                        o_ref, oacc, *, n16, n_sel):
    qi = pl.program_id(1)
    g = pl.program_id(2)
    rows = GROUP * TQ
    lane = lax.broadcasted_iota(jnp.int32, (1, KV_W), 1)
    gmask = (lane // HEAD_DIM) == g

    q = q_ref[0]
    zero = jnp.zeros((), BF16)
    qg = jnp.concatenate(
        [jnp.where(gmask, q[:, z * KV_W:(z + 1) * KV_W], zero) for z in range(GROUP)], axis=0)
    slope = jnp.concatenate(
        [jnp.full((TQ, 1), slopes_ref[g * GROUP + z], F32) for z in range(GROUP)], axis=0)
    qpos_t = qi * TQ + lax.broadcasted_iota(jnp.int32, (TQ, 1), 0)
    qpos = jnp.concatenate([qpos_t] * GROUP, axis=0)

    gates = gate_ref[0]
    glane = lax.broadcasted_iota(jnp.int32, (TQ, LANES), 1)

    def gate_col(branch):
        cols = [jnp.sum(jnp.where(glane == branch * N_HEADS + g * GROUP + z, gates, 0.0),
                        axis=-1, keepdims=True) for z in range(GROUP)]
        return jnp.concatenate(cols, axis=0)

    kc = kc_ref[0, 0].astype(BF16)
    vc = vc_ref[0, 0].astype(BF16)
    cend = lax.broadcasted_iota(jnp.int32, (1, n16), 1) * CMP_STRIDE + (L_CMP - 1)
    dist_c = qpos - cend
    s_c = _dot_nt(qg, kc) - slope * dist_c.astype(F32)
    p_c = _masked_softmax(s_c, dist_c >= 0)
    out = gate_col(0) * _dot(p_c.astype(BF16), vc)

    psum = p_c[0:TQ]
    for z in range(1, GROUP):
        psum = psum + p_c[z * TQ:(z + 1) * TQ]
    imp = _dot_f32_exact_rhs(psum, ov_ref[...], TQ)
    sel = _select_mask(imp, qpos_t, n_sel, LANES, TQ).astype(BF16)

    kpos0 = lax.broadcasted_iota(jnp.int32, (1, CHUNK), 1)
    blk_r = lax.broadcasted_iota(jnp.int32, (LANES, CHUNK), 0)
    blk_half = lax.broadcasted_iota(jnp.int32, (LANES, CHUNK), 1) // SEL_BLK

    def attend(c, carry, k_ref, v_ref, use_sel):
        m, l, acc = carry
        start = pl.multiple_of(c * CHUNK, CHUNK)
        kch = k_ref[0, pl.ds(start, CHUNK), :]
        vch = v_ref[0, pl.ds(start, CHUNK), :]
        dist = qpos - (c * CHUNK + kpos0)
        if use_sel:
            pick = jnp.where(blk_r == c * (CHUNK // SEL_BLK) + blk_half, 1.0, 0.0).astype(BF16)
            msel = _dot(sel, pick)
            msel = jnp.concatenate([msel] * GROUP, axis=0)
            valid = (msel > 0.5) & (dist >= 0)
        else:
            valid = (dist >= 0) & (dist < WINDOW)
        s = jnp.where(valid, _dot_nt(qg, kch) - slope * dist.astype(F32), NEG)
        m_new = jnp.maximum(m, jnp.max(s, axis=-1, keepdims=True))
        alpha = jnp.exp(m - m_new)
        e = jnp.where(valid, jnp.exp(s - m_new), 0.0)
        l = alpha * l + jnp.sum(e, axis=-1, keepdims=True)
        acc = alpha * acc + _dot(e.astype(BF16), vch)
        return m_new, l, acc

    init = (jnp.full((rows, 1), NEG, F32), jnp.zeros((rows, 1), F32), jnp.zeros((rows, KV_W), F32))
    _, l, acc = lax.fori_loop(0, qi + 1, lambda c, cr: attend(c, cr, ks_ref, vs_ref, True), init)
    out = out + (gate_col(1) / jnp.maximum(l, 1e-30)) * acc
    first = jnp.maximum(qi - WINDOW // CHUNK, 0)
    _, l, acc = lax.fori_loop(first, qi + 1, lambda c, cr: attend(c, cr, kw_ref, vw_ref, False), init)
    out = out + (gate_col(2) / jnp.maximum(l, 1e-30)) * acc

    out = jnp.where(gmask, out, 0.0)

    @pl.when(g == 0)
    def _():
        oacc[...] = out

    @pl.when(g > 0)
    def _():
        oacc[...] = oacc[...] + out

    @pl.when(g == N_KV - 1)
    def _():
        o_ref[0] = jnp.concatenate(
            [oacc[z * TQ:(z + 1) * TQ, :] for z in range(GROUP)], axis=1).astype(BF16)


def _attn_prompt(slopes, q, gates, cmp, kvb, overlap, n_sel):
    bsz, seq, _ = q.shape
    n16 = cmp.shape[2]
    assert seq % TQ == 0 and TQ == CHUNK
    full = lambda col: pl.BlockSpec((1, seq, KV_W), lambda b, i, g: (b, 0, col))
    return pl.pallas_call(
        functools.partial(_attn_prompt_kernel, n16=n16, n_sel=n_sel),
        grid=(bsz, seq // TQ, N_KV),
        in_specs=[pl.BlockSpec(memory_space=pltpu.SMEM),
                  pl.BlockSpec((1, TQ, N_Q_COLS), lambda b, i, g: (b, i, 0)),
                  pl.BlockSpec((1, TQ, LANES), lambda b, i, g: (b, i, 0)),
                  pl.BlockSpec((1, 1, n16, KV_W), lambda b, i, g: (0, b, 0, 0)),
                  pl.BlockSpec((1, 1, n16, KV_W), lambda b, i, g: (1, b, 0, 0)),
                  full(2), full(3), full(4), full(5),
                  pl.BlockSpec(overlap.shape, lambda b, i, g: (0, 0))],
        out_specs=pl.BlockSpec((1, TQ, N_Q_COLS), lambda b, i, g: (b, i, 0)),
        out_shape=jax.ShapeDtypeStruct((bsz, seq, N_Q_COLS), BF16),
        scratch_shapes=[pltpu.VMEM((GROUP * TQ, KV_W), F32)],
        compiler_params=_params("parallel", "arbitrary", "arbitrary"),
        name="attn_prompt",
    )(slopes, q, gates, cmp, cmp, kvb, kvb, kvb, kvb, overlap)


def _attn_sample_kernel(pt_ref, slopes_ref, q_ref, gate_ref, kc_ref, vc_ref, kvn_ref, win_ref, ov_ref, *refs,
                        tn, n16, n_sel, n_steps, past_len, wbuf, sel_w):
    del pt_ref
    kpages = refs[:PAGES_PER_STEP]
    vpages = refs[PAGES_PER_STEP:2 * PAGES_PER_STEP]
    o_ref, qall_ref, tab_ref, m_ref, l_ref, acc_ref, oc_ref = refs[2 * PAGES_PER_STEP:]
    st = pl.program_id(1)
    rows = N_HEADS * tn
    keys_step = PAGES_PER_STEP * PAGE_SIZE
    blocks_step = keys_step // SEL_BLK
    lane = lax.broadcasted_iota(jnp.int32, (1, KV_W), 1)

    def per_head(fn):
        return jnp.concatenate([fn(g, z) for g in range(N_KV) for z in range(GROUP)], axis=0)

    slope = per_head(lambda g, z: jnp.full((tn, 1), slopes_ref[g * GROUP + z], F32))
    qpos = past_len + lax.broadcasted_iota(jnp.int32, (rows, 1), 0) % tn

    @pl.when(st == 0)
    def _():
        q = q_ref[0].astype(F32)
        qall = per_head(lambda g, z: jnp.where((lane // HEAD_DIM) == g, q[:, z * KV_W:(z + 1) * KV_W], 0.0))
        qall = qall.astype(BF16)
        qall_ref[...] = qall
        kc = kc_ref[0, 0].astype(BF16)
        vc = vc_ref[0, 0].astype(BF16)
        cend = lax.broadcasted_iota(jnp.int32, (1, n16), 1) * CMP_STRIDE + (L_CMP - 1)
        dist_c = qpos - cend
        s_c = _dot_nt(qall, kc) - slope * dist_c.astype(F32)
        p_c = _masked_softmax(s_c, dist_c >= 0)
        oc_ref[...] = _dot(p_c.astype(BF16), vc)
        psum = []
        for g in range(N_KV):
            acc = p_c[g * GROUP * tn:g * GROUP * tn + tn]
            for z in range(1, GROUP):
                acc = acc + p_c[(g * GROUP + z) * tn:(g * GROUP + z + 1) * tn]
            psum.append(acc)
        psum = jnp.concatenate(psum, axis=0)
        imp = _dot_f32_exact_rhs(psum, ov_ref[...], N_KV * tn)
        qpos_g = past_len + lax.broadcasted_iota(jnp.int32, (N_KV * tn, 1), 0) % tn
        sel = _select_mask(imp, qpos_g, n_sel, sel_w, N_KV * tn)
        sel = per_head(lambda g, z: sel[g * tn:(g + 1) * tn])
        for i in range(n_steps + 1):
            piece = sel[:, i * blocks_step:min((i + 1) * blocks_step, sel_w)]
            pad = LANES - piece.shape[1]
            if pad:
                piece = jnp.concatenate([piece, jnp.zeros((rows, pad), F32)], axis=1)
            tab_ref[i] = piece
        m_ref[...] = jnp.full((rows, 1), NEG, F32)
        l_ref[...] = jnp.zeros((rows, 1), F32)
        acc_ref[...] = jnp.zeros((rows, KV_W), F32)

    qall = qall_ref[...]

    def online(s, valid, v):
        m = m_ref[...]
        s = jnp.where(valid, s, NEG)
        m_new = jnp.maximum(m, jnp.max(s, axis=-1, keepdims=True))
        alpha = jnp.exp(m - m_new)
        e = jnp.where(valid, jnp.exp(s - m_new), 0.0)
        l_ref[...] = alpha * l_ref[...] + jnp.sum(e, axis=-1, keepdims=True)
        acc_ref[...] = alpha * acc_ref[...] + _dot(e.astype(BF16), v)
        m_ref[...] = m_new

    kb = jnp.concatenate([p[0] for p in kpages], axis=0).astype(BF16)
    vb = jnp.concatenate([p[0] for p in vpages], axis=0).astype(BF16)
    tab = tab_ref[st]
    half = lax.broadcasted_iota(jnp.int32, (1, LANES), 1) < SEL_BLK
    msel = jnp.concatenate(
        [jnp.where(half, tab[:, 2 * c:2 * c + 1], tab[:, 2 * c + 1:2 * c + 2])
         for c in range(keys_step // LANES)], axis=1)
    kpos = st * keys_step + lax.broadcasted_iota(jnp.int32, (1, keys_step), 1)
    dist = qpos - kpos
    online(_dot_nt(qall, kb) - slope * dist.astype(F32), (msel > 0.5) & (dist >= 0), vb)

    @pl.when(st == n_steps - 1)
    def _():
        kvn = kvn_ref[0].astype(F32)
        zpad = jnp.zeros((LANES - tn, KV_W), F32)

        def new_rows(ch):
            return jnp.concatenate([kvn[:, ch * KV_W:(ch + 1) * KV_W], zpad], axis=0).astype(BF16)

        kpos_n = past_len + lax.broadcasted_iota(jnp.int32, (1, LANES), 1)
        dist_n = qpos - kpos_n
        col = (past_len // SEL_BLK) % blocks_step
        seln = tab_ref[n_steps][:, col:col + 1]
        online(_dot_nt(qall, new_rows(2)) - slope * dist_n.astype(F32),
               (seln > 0.5) & (dist_n >= 0), new_rows(3))
        o_s = acc_ref[...] / jnp.maximum(l_ref[...], 1e-30)

        win = win_ref[0]
        kw = jnp.concatenate([win[:, :KV_W].astype(BF16), new_rows(4)], axis=0)
        vw = jnp.concatenate([win[:, KV_W:].astype(BF16), new_rows(5)], axis=0)
        kpos_w = past_len - wbuf + lax.broadcasted_iota(jnp.int32, (1, wbuf + LANES), 1)
        dist_w = qpos - kpos_w
        s_w = _dot_nt(qall, kw) - slope * dist_w.astype(F32)
        p_w = _masked_softmax(s_w, (dist_w >= 0) & (dist_w < WINDOW))
        o_w = _dot(p_w.astype(BF16), vw)

        gates = gate_ref[0]

        def gate_col(branch):
            return per_head(lambda g, z: gates[:, branch * N_HEADS + g * GROUP + z:
                                               branch * N_HEADS + g * GROUP + z + 1])

        out = gate_col(0) * oc_ref[...] + gate_col(1) * o_s + gate_col(2) * o_w
        cols = []
        for z in range(GROUP):
            oz = jnp.zeros((tn, KV_W), F32)
            for g in range(N_KV):
                r0 = (g * GROUP + z) * tn
                oz = oz + jnp.where((lane // HEAD_DIM) == g, out[r0:r0 + tn], 0.0)
            cols.append(oz)
        o_ref[0] = jnp.concatenate(cols, axis=1).astype(BF16)


def _attn_sample(slopes, page_table, q, gates, cmp, kvn, win_state, overlap, cache, n_sel):
    bsz, tn, _ = q.shape
    n_pages = page_table.shape[1]
    assert n_pages % PAGES_PER_STEP == 0 and tn <= SEL_BLK and tn % SUBLANES == 0
    n_steps = n_pages // PAGES_PER_STEP
    past_len = n_pages * PAGE_SIZE
    n16 = cmp.shape[2]
    wbuf = win_state.shape[1]
    sel_w = overlap.shape[1]
    blocks_step = PAGES_PER_STEP * PAGE_SIZE // SEL_BLK
    assert n_sel == past_len // SEL_BLK + 1 and sel_w >= n_sel and blocks_step <= LANES
    rows = N_HEADS * tn

    def page_spec(i, ch):
        return pl.BlockSpec((1, PAGE_SIZE, KV_W),
                            lambda b, st, pt: (pt[b, st * PAGES_PER_STEP + i], 0, ch))

    per_b = lambda shape: pl.BlockSpec((1,) + shape, lambda b, st, pt: (b,) + (0,) * len(shape))
    grid_spec = pltpu.PrefetchScalarGridSpec(
        num_scalar_prefetch=1,
        grid=(bsz, n_steps),
        in_specs=[pl.BlockSpec(memory_space=pltpu.SMEM),
                  per_b((tn, N_Q_COLS)), per_b((tn, LANES)),
                  pl.BlockSpec((1, 1, n16, KV_W), lambda b, st, pt: (0, b, 0, 0)),
                  pl.BlockSpec((1, 1, n16, KV_W), lambda b, st, pt: (1, b, 0, 0)),
                  per_b((tn, N_KV_COLS)), per_b((wbuf, 2 * KV_W)),
                  pl.BlockSpec(overlap.shape, lambda b, st, pt: (0, 0))]
                 + [page_spec(i, 2) for i in range(PAGES_PER_STEP)]
                 + [page_spec(i, 3) for i in range(PAGES_PER_STEP)],
        out_specs=per_b((tn, N_Q_COLS)),
        scratch_shapes=[pltpu.VMEM((rows, KV_W), BF16),
                        pltpu.VMEM((n_steps + 1, rows, LANES), F32),
                        pltpu.VMEM((rows, 1), F32), pltpu.VMEM((rows, 1), F32),
                        pltpu.VMEM((rows, KV_W), F32), pltpu.VMEM((rows, KV_W), F32)],
    )
    return pl.pallas_call(
        functools.partial(_attn_sample_kernel, tn=tn, n16=n16, n_sel=n_sel, n_steps=n_steps,
                          past_len=past_len, wbuf=wbuf, sel_w=sel_w),
        grid_spec=grid_spec,
        out_shape=jax.ShapeDtypeStruct((bsz, tn, N_Q_COLS), BF16),
        compiler_params=_params("parallel", "arbitrary"),
        name="attn_sample",
    )(page_table, slopes, q, gates, cmp, cmp, kvn, win_state, overlap,
      *([cache] * (2 * PAGES_PER_STEP)))


def _overlap_matrix(n16, n_sel, width):
    cstart = jnp.arange(n16)[:, None] * CMP_STRIDE
    sstart = jnp.arange(width)[None, :] * SEL_BLK
    ov = jnp.clip(jnp.minimum(cstart + L_CMP, sstart + SEL_BLK) - jnp.maximum(cstart, sstart),
                  0, L_CMP).astype(F32) / L_CMP
    return jnp.where(jnp.arange(width)[None, :] < n_sel, ov, 0.0).astype(BF16)


def _round_up(x, m):
    return -(-x // m) * m


def kernel(x_prompt, x_sample, cache_kv, page_table, state_kv_win, state_conv, norm_g, w_ff_in, w_ff_out,
           w_pw1, w_dw, b_dw, ln_g, ln_b, w_pw2, g_kv, w_kv, w_cmp1, b_cmp1, w_cmp2, b_cmp2, pe_cmp,
           w_qg, w_o, g_final):
    depth = norm_g.shape[0]
    n_a = w_pw1.shape[0]
    assert depth == 2 and n_a == 1

    ff_in = w_ff_in.astype(BF16)
    ff_out = w_ff_out.astype(BF16)
    pw1 = w_pw1.astype(BF16)
    pw2 = w_pw2.astype(BF16)
    wkv = w_kv.astype(BF16)
    wq = w_qg[0][:, :N_Q_COLS].reshape(D_MODEL, N_KV, GROUP, HEAD_DIM).transpose(0, 2, 1, 3)
    wqg = jnp.concatenate(
        [wq.reshape(D_MODEL, N_Q_COLS), w_qg[0][:, N_Q_COLS:],
         jnp.zeros((D_MODEL, LANES - N_GATE), F32)], axis=1).astype(BF16)
    wo = w_o[0].reshape(N_KV, GROUP, HEAD_DIM, D_MODEL).transpose(1, 0, 2, 3)
    wo = wo.reshape(N_Q_COLS, D_MODEL).astype(BF16)

    w1cat = jnp.concatenate([w_cmp1[:, :CMP_STRIDE], w_cmp1[:, CMP_STRIDE:]], axis=-1)
    eye = jnp.eye(N_KV, dtype=F32)
    eye_half = jnp.eye(LANES // HEAD_DIM, dtype=F32)
    wpair = jnp.einsum('csdh,gk->csgdkh', w1cat, eye_half)
    wpair = wpair.reshape(2, CMP_STRIDE, LANES, FS_HALF).astype(BF16)
    w2bd = jnp.einsum('chd,gk->cghkd', w_cmp2, eye).reshape(2, N_KV * CMP_HID, KV_W).astype(BF16)
    b2cat = jnp.tile(b_cmp2, (1, N_KV)).reshape(2, 1, KV_W)
    pe_flat = pe_cmp.reshape(2, 1, L_CMP * HEAD_DIM)
    w1_flat = w_cmp1.reshape(2, L_CMP * HEAD_DIM, CMP_HID)
    b1 = b_cmp1.reshape(2, 1, CMP_HID)
    cmp_w = (wpair, pe_flat, w1_flat, b1, w2bd, b2cat)

    slopes = jnp.exp2(-8.0 * (jnp.arange(N_HEADS, dtype=F32) + 1.0) / N_HEADS)

    def trunk(x, conv_prev, attend):
        bsz, seq, _ = x.shape
        h = x.reshape(bsz * seq, D_MODEL)
        h = _ffn(h, norm_g[0, 0], ff_in[0, 0], ff_out[0, 0])
        h, conv_new = _conv(h.reshape(bsz, seq, D_MODEL), conv_prev, norm_g[0, 1], pw1[0], w_dw[0],
                            b_dw[0], ln_g[0], ln_b[0], pw2[0])
        h = _ffn(h.reshape(bsz * seq, D_MODEL), norm_g[0, 2], ff_in[0, 1], ff_out[0, 1])
        kv, win, kvb = _kv_rows(h, g_kv, wkv)
        h = _ffn(h, norm_g[1, 0], ff_in[1, 0], ff_out[1, 0])
        q, gates = _qg_proj(h, norm_g[1, 1], wqg)
        o = attend(q.reshape(bsz, seq, N_Q_COLS), gates.reshape(bsz, seq, LANES), kv, win, kvb)
        h = _oproj(o.reshape(bsz * seq, N_Q_COLS), wo, h)
        y = _ffn(h, norm_g[1, 2], ff_in[1, 1], ff_out[1, 1], g_final)
        return (y.reshape(bsz, seq, D_MODEL), conv_new[None],
                kv.reshape(bsz, seq, N_PAGED, N_KV, HEAD_DIM),
                win.reshape(bsz, seq, N_KV_ROWS - N_PAGED, N_KV, HEAD_DIM))

    bp, tp, _ = x_prompt.shape
    assert tp % PAGE_SIZE == 0

    def attend_prompt(q, gates, kv, win, kvb):
        pages_per_seq = tp // PAGE_SIZE
        ident = jnp.arange(bp * pages_per_seq, dtype=jnp.int32).reshape(bp, pages_per_seq)
        cmp = _compress(kv.reshape(bp * pages_per_seq, PAGE_SIZE, N_PAGED_COLS), ident, 0, *cmp_w)
        n_sel = -(-tp // SEL_BLK)
        assert n_sel <= LANES
        overlap = _overlap_matrix(cmp.shape[2], n_sel, LANES)
        return _attn_prompt(slopes, q, gates, cmp, kvb.reshape(bp, tp, N_KV_COLS), overlap, n_sel)

    conv0 = jnp.zeros((bp, CONV_W - 1, D_MODEL), F32)
    y_prompt, conv_prompt, kv_prompt, win_rows = trunk(x_prompt, conv0, attend_prompt)
    win_prompt = win_rows[:, tp - min(WINDOW, tp):]

    bs, tn, _ = x_sample.shape
    n_pool = cache_kv.shape[0]
    past_len = page_table.shape[1] * PAGE_SIZE
    cache2d = cache_kv.reshape(n_pool, PAGE_SIZE, N_PAGED_COLS)

    def attend_sample(q, gates, kv, win, kvb):
        cmp = _compress(cache2d, page_table, 0, *cmp_w)
        n_sel = -(-(past_len + tn) // SEL_BLK)
        overlap = _overlap_matrix(cmp.shape[2], n_sel, _round_up(n_sel, LANES))
        win_state = state_kv_win.reshape(bs, state_kv_win.shape[1], 2 * KV_W)
        return _attn_sample(slopes, page_table, q, gates, cmp, kvb.reshape(bs, tn, N_KV_COLS), win_state,
                            overlap, cache2d, n_sel)

    y_sample, conv_sample, kv_sample, win_new = trunk(x_sample, state_conv[0], attend_sample)
    win_all = jnp.concatenate([state_kv_win, win_new], axis=1)
    keep = min(WINDOW, past_len + tn)
    win_sample = win_all[:, win_all.shape[1] - keep:]

    return (y_prompt, y_sample, kv_prompt, kv_sample, win_prompt, win_sample, conv_prompt, conv_sample)
```
